```python
import math
import jax
import jax.numpy as jnp
from jax import lax
import numpy as np

D_MODEL = 1024
BATCH = 16
SEQ = 256
DEPTH = 4
DEC_BATCH = 4
DEC_SEQ = 2048
PAST_LEN = 512

GRID_W = 64
SSD_INNER = 2 * D_MODEL
SSD_HEAD_DIM = 64
SSD_HEADS = SSD_INNER // SSD_HEAD_DIM
SSD_GROUPS = 4
SSD_STATE = 128
SSD_CONV = 5
SSD_CHUNK = 128
SSD_BC = SSD_GROUPS * SSD_STATE
SSD_CONV_CH = SSD_INNER + 2 * SSD_BC
POOL_WIDTH = D_MODEL
POOL_WINDOWS = (2, 4, 8, 16)
POOL_GROUP = POOL_WIDTH // len(POOL_WINDOWS)
GMLP_WIDTH = D_MODEL
GMLP_GROUPS = 4
GMLP_GROUP = GMLP_WIDTH // GMLP_GROUPS
GMLP_CHUNK = 128
N_BRANCH = 3
D_FF = -(-8 * D_MODEL // (3 * 256)) * 256
IN_SIZES = (SSD_INNER, SSD_CONV_CH, 2 * SSD_HEADS, POOL_WIDTH, GMLP_WIDTH, GMLP_WIDTH, N_BRANCH * D_MODEL)
IN_COLS = sum(IN_SIZES)
RMS_EPS = 1e-6
POS_BASE = 10000.0

kernel_name = 'bidir_ssd_pool_gmlp_flow_step'


def rms_norm(x, g):
    xf = x.astype(jnp.float32)
    y = xf * lax.rsqrt(jnp.mean(xf * xf, axis=-1, keepdims=True) + RMS_EPS)
    return (y * g.astype(jnp.float32)).astype(x.dtype)


def grouped_rms_norm(x, g, n_groups):
    shp = x.shape
    xf = x.astype(jnp.float32).reshape(shp[:-1] + (n_groups, shp[-1] // n_groups))
    y = xf * lax.rsqrt(jnp.mean(xf * xf, axis=-1, keepdims=True) + RMS_EPS)
    return (y.reshape(shp) * g.astype(jnp.float32)).astype(x.dtype)


def grid_pos_embed(n_tokens, dim):
    rows = n_tokens // GRID_W
    quarter = dim // 4
    omega = jnp.exp(-math.log(POS_BASE) * jnp.arange(quarter, dtype=jnp.float32) / quarter)
    row = jnp.repeat(jnp.arange(rows, dtype=jnp.float32), GRID_W)
    col = jnp.tile(jnp.arange(GRID_W, dtype=jnp.float32), rows)

    def axis_emb(pos):
        ang = pos[:, None] * omega[None, :]
        return jnp.concatenate([jnp.sin(ang), jnp.cos(ang)], axis=-1)

    return jnp.concatenate([axis_emb(row), axis_emb(col)], axis=-1)


def centred_depthwise_conv(x, w, b):
    k, ch = w.shape
    y = lax.conv_general_dilated(x, w.astype(x.dtype)[:, None, :], window_strides=(1,),
                                 padding=[(k // 2, k // 2)], dimension_numbers=('NWC', 'WIO', 'NWC'),
                                 feature_group_count=ch)
    return y + b.astype(x.dtype)


def ssd_chunked_scan(x, dt, a, b_in, c_in, s0):
    bsz, seq, n_heads, hd = x.shape
    n_groups, d_state = b_in.shape[2], b_in.shape[3]
    hpg = n_heads // n_groups
    q = SSD_CHUNK
    nc = seq // q
    f32 = jnp.float32
    xdt = (x.astype(f32) * dt[..., None]).reshape(bsz, nc, q, n_groups, hpg, hd)
    log_a = (dt * a).reshape(bsz, nc, q, n_groups, hpg)
    acum = jnp.cumsum(log_a, axis=2)
    br = b_in.astype(f32).reshape(bsz, nc, q, n_groups, d_state)
    cr = c_in.astype(f32).reshape(bsz, nc, q, n_groups, d_state)
    acum_t = jnp.moveaxis(acum, 2, -1)
    seg = acum_t[..., :, None] - acum_t[..., None, :]
    pos = jnp.arange(q)
    lower = pos[:, None] >= pos[None, :]
    decay = jnp.exp(jnp.where(lower, seg, -jnp.inf))
    cb = jnp.einsum('bcign,bcjgn->bcgij', cr, br)
    y_diag = jnp.einsum('bcgij,bcgkij,bcjgkp->bcigkp', cb, decay, xdt)
    decay_end = jnp.exp(acum[:, :, -1:] - acum)
    chunk_states = jnp.einsum('bcjgn,bcjgk,bcjgkp->bcgkpn', br, decay_end, xdt)
    chunk_decay = jnp.exp(acum[:, :, -1])

    def step(s, inp):
        st, dec = inp
        return s * dec[..., None, None] + st, s

    s_init = s0.astype(f32).reshape(bsz, n_groups, hpg, hd, d_state)
    s_last, s_enter = lax.scan(step, s_init, (jnp.moveaxis(chunk_states, 1, 0), jnp.moveaxis(chunk_decay, 1, 0)))
    s_enter = jnp.moveaxis(s_enter, 0, 1)
    y_off = jnp.einsum('bcign,bcigk,bcgkpn->bcigkp', cr, jnp.exp(acum), s_enter)
    y = (y_diag + y_off).reshape(bsz, seq, n_heads, hd)
    return y, s_last.reshape(bsz, n_heads, hd, d_state).astype(s0.dtype)


def ssd_mixer(z, xbc, dt_raw, s0f, s0b, conv_w, conv_b, dt_bias, a_log, d_skip, g_norm):
    bsz, seq, _ = xbc.shape
    xbc = jax.nn.silu(centred_depthwise_conv(xbc, conv_w, conv_b))
    xs = xbc[..., :SSD_INNER].reshape(bsz, seq, SSD_HEADS, SSD_HEAD_DIM)
    bm = xbc[..., SSD_INNER:SSD_INNER + SSD_BC].reshape(bsz, seq, SSD_GROUPS, SSD_STATE)
    cm = xbc[..., SSD_INNER + SSD_BC:].reshape(bsz, seq, SSD_GROUPS, SSD_STATE)
    dt = jax.nn.softplus(dt_raw.astype(jnp.float32).reshape(bsz, seq, 2, SSD_HEADS) + dt_bias.astype(jnp.float32))
    a = -jnp.exp(a_log.astype(jnp.float32))
    y_f, s_f = ssd_chunked_scan(xs, dt[:, :, 0], a[0], bm, cm, s0f)
    flip = lambda t: jnp.flip(t, axis=1)
    y_b, s_b = ssd_chunked_scan(flip(xs), flip(dt[:, :, 1]), a[1], flip(bm), flip(cm), s0b)
    y = y_f + flip(y_b) + xs.astype(jnp.float32) * d_skip.astype(jnp.float32)[:, None]
    y = y.astype(z.dtype).reshape(bsz, seq, SSD_INNER) * jax.nn.silu(z)
    return grouped_rms_norm(y, g_norm, SSD_GROUPS), s_f, s_b


def pool_mixer(p, w_pool, pool_scale):
    bsz, seq, width = p.shape
    pf = p.astype(jnp.float32)
    cs = jnp.concatenate([jnp.zeros((bsz, 1, width), jnp.float32), jnp.cumsum(pf, axis=1)], axis=1)
    t = jnp.arange(seq)
    outs = []
    for gi, w in enumerate(POOL_WINDOWS):
        lo = jnp.clip(t - w // 2, 0, seq)
        hi = jnp.clip(t + w // 2, 0, seq)
        csg = cs[:, :, gi * POOL_GROUP:(gi + 1) * POOL_GROUP]
        mean = (csg[:, hi] - csg[:, lo]) / (hi - lo).astype(jnp.float32)[None, :, None]
        outs.append(mean - pf[:, :, gi * POOL_GROUP:(gi + 1) * POOL_GROUP])
    pooled = jnp.stack(outs, axis=2)
    mixed = jnp.einsum('blgc,gcd->blgd', pooled, w_pool.astype(jnp.float32)).reshape(bsz, seq, width)
    return (mixed * pool_scale.astype(jnp.float32)).astype(p.dtype)


def chunk_gmlp(u, v, g_sgu, w_spatial, b_spatial):
    bsz, seq, _ = v.shape
    v = rms_norm(v, g_sgu)
    vr = v.reshape(bsz, seq // GMLP_CHUNK, GMLP_CHUNK, GMLP_GROUPS, GMLP_GROUP)
    sv = jnp.einsum('gij,bcjgd->bcigd', w_spatial, vr) + b_spatial.T[None, None, :, :, None]
    return u * sv.reshape(bsz, seq, GMLP_WIDTH)


def trunk_layer(x, mod, s0f, s0b, lp):
    shift1, scale1, gate1, shift2, scale2, gate2 = jnp.split(mod, 6, axis=-1)
    h = rms_norm(x, lp['g_norm1']) * (1 + scale1) + shift1
    offs = np.cumsum(IN_SIZES)[:-1].tolist()
    z, xbc, dt_raw, p, u, v, gate_logits = jnp.split(h @ lp['w_in'], offs, axis=-1)
    y_ssd, s_f, s_b = ssd_mixer(z, xbc, dt_raw, s0f, s0b, lp['conv_w'], lp['conv_b'], lp['dt_bias'],
                                lp['a_log'], lp['d_skip'], lp['g_ssd'])
    y_pool = pool_mixer(p, lp['w_pool'], lp['pool_scale'])
    y_gmlp = chunk_gmlp(u, v, lp['g_sgu'], lp['w_spatial'], lp['b_spatial'])
    g_a, g_b, g_c = jnp.split(jax.nn.sigmoid(gate_logits), N_BRANCH, axis=-1)
    merged = (g_a * (y_ssd @ lp['w_br_ssd']) + g_b * (y_pool @ lp['w_br_pool'])
              + g_c * (y_gmlp @ lp['w_br_gmlp']))
    x = x + gate1 * (merged @ lp['w_out'])
    h2 = rms_norm(x, lp['g_norm2']) * (1 + scale2) + shift2
    f_gate, f_up = jnp.split(h2 @ lp['w_ffn_in'], 2, axis=-1)
    x = x + gate2 * ((jax.nn.silu(f_gate) * f_up) @ lp['w_ffn_out'])
    return x, s_f, s_b


def setup_inputs(seed: int = 0) -> dict:
    key = jax.random.key(seed)
    ks = jax.random.split(key, 32)
    f32 = jnp.float32
    nrm = lambda k, shape, s: jax.random.normal(k, shape, f32) * s
    state_shape = (DEC_BATCH, DEPTH, SSD_HEADS, SSD_HEAD_DIM, SSD_STATE)
    dt0 = jnp.exp(jax.random.uniform(ks[12], (DEPTH, 2, SSD_HEADS), f32, math.log(1e-3), math.log(1e-1)))
    return {
        'x_prompt': nrm(ks[0], (BATCH, SEQ, D_MODEL), 1.0),
        'x_sample': nrm(ks[1], (DEC_BATCH, DEC_SEQ, D_MODEL), 1.0),
        'state_ssd_fwd': nrm(ks[2], state_shape, 0.5),
        'state_ssd_bwd': nrm(ks[3], state_shape, 0.5),
        'c': nrm(ks[4], (DEC_BATCH, D_MODEL), 1.0),
        'c_ctx': nrm(ks[5], (D_MODEL,), 1.0),
        'w_ada': nrm(ks[6], (DEPTH, D_MODEL, 6 * D_MODEL), 0.5 * D_MODEL ** -0.5),
        'b_ada': nrm(ks[7], (DEPTH, 6 * D_MODEL), 0.02),
        'g_norm1': 1.0 + nrm(ks[8], (DEPTH, D_MODEL), 0.02),
        'w_in': nrm(ks[9], (DEPTH, D_MODEL, IN_COLS), D_MODEL ** -0.5),
        'conv_w': nrm(ks[10], (DEPTH, SSD_CONV, SSD_CONV_CH), SSD_CONV ** -0.5),
        'conv_b': nrm(ks[11], (DEPTH, SSD_CONV_CH), 0.02),
        'dt_bias': dt0 + jnp.log(-jnp.expm1(-dt0)),
        'a_log': jnp.log(jax.random.uniform(ks[13], (DEPTH, 2, SSD_HEADS), f32, 1.0, 16.0)),
        'd_skip': 1.0 + nrm(ks[14], (DEPTH, SSD_HEADS), 0.02),
        'g_ssd': 1.0 + nrm(ks[15], (DEPTH, SSD_INNER), 0.02),
        'w_br_ssd': nrm(ks[16], (DEPTH, SSD_INNER, D_MODEL), SSD_INNER ** -0.5),
        'w_pool': nrm(ks[17], (DEPTH, len(POOL_WINDOWS), POOL_GROUP, POOL_GROUP), POOL_GROUP ** -0.5),
        'pool_scale': 1.0 + nrm(ks[18], (DEPTH, POOL_WIDTH), 0.02),
        'w_br_pool': nrm(ks[19], (DEPTH, POOL_WIDTH, D_MODEL), POOL_WIDTH ** -0.5),
        'g_sgu': 1.0 + nrm(ks[20], (DEPTH, GMLP_WIDTH), 0.02),
        'w_spatial': nrm(ks[21], (DEPTH, GMLP_GROUPS, GMLP_CHUNK, GMLP_CHUNK), GMLP_CHUNK ** -0.5),
        'b_spatial': 1.0 + nrm(ks[22], (DEPTH, GMLP_GROUPS, GMLP_CHUNK), 0.02),
        'w_br_gmlp': nrm(ks[23], (DEPTH, GMLP_WIDTH, D_MODEL), GMLP_WIDTH ** -0.5),
        'w_out': nrm(ks[24], (DEPTH, D_MODEL, D_MODEL), D_MODEL ** -0.5),
        'g_norm2': 1.0 + nrm(ks[25], (DEPTH, D_MODEL), 0.02),
        'w_ffn_in': nrm(ks[26], (DEPTH, D_MODEL, 2 * D_FF), D_MODEL ** -0.5),
        'w_ffn_out': nrm(ks[27], (DEPTH, D_FF, D_MODEL), D_FF ** -0.5),
        'g_final': 1.0 + nrm(ks[28], (D_MODEL,), 0.02),
    }


def reference(x_prompt, x_sample, state_ssd_fwd, state_ssd_bwd, c, c_ctx, w_ada, b_ada, g_norm1, w_in,
              conv_w, conv_b, dt_bias, a_log, d_skip, g_ssd, w_br_ssd, w_pool, pool_scale, w_br_pool,
              g_sgu, w_spatial, b_spatial, w_br_gmlp, w_out, g_norm2, w_ffn_in, w_ffn_out, g_final):
    pos = grid_pos_embed(x_sample.shape[1], x_sample.shape[2])
    xp = x_prompt
    xs = x_sample + pos.astype(x_sample.dtype)[None]
    zero_state = jnp.zeros((x_prompt.shape[0], SSD_HEADS, SSD_HEAD_DIM, SSD_STATE), x_prompt.dtype)
    fwd_states, bwd_states = [], []
    for l in range(DEPTH):
        lp = {'g_norm1': g_norm1[l], 'w_in': w_in[l], 'conv_w': conv_w[l], 'conv_b': conv_b[l],
              'dt_bias': dt_bias[l], 'a_log': a_log[l], 'd_skip': d_skip[l], 'g_ssd': g_ssd[l],
              'w_br_ssd': w_br_ssd[l], 'w_pool': w_pool[l], 'pool_scale': pool_scale[l],
              'w_br_pool': w_br_pool[l], 'g_sgu': g_sgu[l], 'w_spatial': w_spatial[l],
              'b_spatial': b_spatial[l], 'w_br_gmlp': w_br_gmlp[l], 'w_out': w_out[l],
              'g_norm2': g_norm2[l], 'w_ffn_in': w_ffn_in[l], 'w_ffn_out': w_ffn_out[l]}
        mod_ctx = (jax.nn.silu(c_ctx) @ w_ada[l] + b_ada[l]).reshape(1, 1, -1)
        mod_lat = (jax.nn.silu(c) @ w_ada[l] + b_ada[l])[:, None, :]
        xp, s_f, s_b = trunk_layer(xp, mod_ctx, zero_state, zero_state, lp)
        fwd_states.append(s_f)
        bwd_states.append(s_b)
        xs, _, _ = trunk_layer(xs, mod_lat, state_ssd_fwd[:, l], state_ssd_bwd[:, l], lp)
    y_prompt = rms_norm(xp, g_final)
    y_sample = rms_norm(xs, g_final)
    new_state_ssd_fwd = jnp.stack(fwd_states, axis=1)
    new_state_ssd_bwd = jnp.stack(bwd_states, axis=1)
    return (y_prompt, y_sample, new_state_ssd_fwd, new_state_ssd_bwd)
```

```python
import functools
import math

import jax
import jax.numpy as jnp
import numpy as np
from jax import lax
from jax.experimental import pallas as pl
from jax.experimental.pallas import tpu as pltpu

F32 = jnp.float32
BF16 = jnp.bfloat16

D_MODEL = 1024
GRID_W = 64
POS_BASE = 10000.0
RMS_EPS = 1e-6
SSD_INNER = 2 * D_MODEL
SSD_HEAD_DIM = 64
SSD_HEADS = SSD_INNER // SSD_HEAD_DIM
SSD_GROUPS = 4
SSD_STATE = 128
SSD_CONV = 5
SSD_BC = SSD_GROUPS * SSD_STATE
SSD_CONV_CH = SSD_INNER + 2 * SSD_BC
HEADS_PER_GROUP = SSD_HEADS // SSD_GROUPS
POOL_WINDOWS = (2, 4, 8, 16)
POOL_GROUP = D_MODEL // len(POOL_WINDOWS)
GMLP_GROUPS = 4
GMLP_GROUP = D_MODEL // GMLP_GROUPS
CHUNK = 128
HALO = 16
D_FF = -(-8 * D_MODEL // (3 * 256)) * 256
N_BRANCH = 3

COL_P, COL_U, COL_V = 3, 4, 5
COL_Z = 3
COL_GATE = 8
PROJ_COLS = 11 * D_MODEL
DT_COLS = 256

V7X_VMEM_LIMIT = 56 * 1024 * 1024


def _silu(x):
    return x / (1.0 + jnp.exp(-x))


def _sigmoid(x):
    return 1.0 / (1.0 + jnp.exp(-x))


def _softplus(x):
    return jnp.maximum(x, 0.0) + jnp.log1p(jnp.exp(-jnp.abs(x)))


def _dot(a, b):
    return jnp.dot(a, b, preferred_element_type=F32)


def _rms(x, eps=RMS_EPS):
    return x * lax.rsqrt(jnp.mean(x * x, axis=-1, keepdims=True) + eps)


def _params(sem, vmem=V7X_VMEM_LIMIT):
    return pltpu.CompilerParams(dimension_semantics=sem, vmem_limit_bytes=vmem)


def _embed_kernel(xp_ref, xs_ref, pos_ref, o_ref, *, n_prompt_tiles):
    i = pl.program_id(0)

    @pl.when(i < n_prompt_tiles)
    def _():
        o_ref[...] = xp_ref[...]

    @pl.when(i >= n_prompt_tiles)
    def _():
        o_ref[...] = xs_ref[...] + pos_ref[...]


def _embed(xp, xs, pos, tm):
    tp, d = xp.shape
    ts = xs.shape[0]
    npt, nst, npos = tp // tm, ts // tm, pos.shape[0] // tm
    return pl.pallas_call(
        functools.partial(_embed_kernel, n_prompt_tiles=npt),
        grid=(npt + nst,),
        in_specs=[
            pl.BlockSpec((tm, d), lambda i: (jnp.minimum(i, npt - 1), 0)),
            pl.BlockSpec((tm, d), lambda i: (jnp.maximum(i - npt, 0), 0)),
            pl.BlockSpec((tm, d), lambda i: (jnp.maximum(i - npt, 0) % npos, 0)),
        ],
        out_specs=pl.BlockSpec((tm, d), lambda i: (i, 0)),
        out_shape=jax.ShapeDtypeStruct((tp + ts, d), F32),
        compiler_params=_params(("parallel",)),
        name="embed",
    )(xp, xs, pos)


def _mod_kernel(c_ref, w_ref, b_ref, o_ref):
    a = _silu(c_ref[...]).astype(BF16)
    o_ref[0] = _dot(a, w_ref[0].astype(BF16)) + b_ref[0]


def _modulation(cvec, w_ada, b_ada, tn=1024):
    depth, d, n = w_ada.shape
    rows = cvec.shape[0]
    return pl.pallas_call(
        _mod_kernel,
        grid=(depth, n // tn),
        in_specs=[
            pl.BlockSpec((rows, d), lambda l, j: (0, 0)),
            pl.BlockSpec((1, d, tn), lambda l, j: (l, 0, j)),
            pl.BlockSpec((1, 1, tn), lambda l, j: (l, 0, j)),
        ],
        out_specs=pl.BlockSpec((1, rows, tn), lambda l, j: (l, 0, j)),
        out_shape=jax.ShapeDtypeStruct((depth, rows, n), F32),
        compiler_params=_params(("parallel", "parallel")),
        name="modulation",
    )(cvec, w_ada, b_ada.reshape(depth, 1, n))


def _in_proj_kernel(x_ref, mod_ref, g_ref, w_ref, wdt_ref, o_ref, dt_ref, h_ref):
    @pl.when(pl.program_id(1) == 0)
    def _():
        y = _rms(x_ref[...]) * g_ref[...]
        h = (y * (1.0 + mod_ref[0, 1:2, :]) + mod_ref[0, 0:1, :]).astype(BF16)
        h_ref[...] = h
        dt_ref[...] = _dot(h, wdt_ref[...])

    o_ref[...] = _dot(h_ref[...], w_ref[...]).astype(o_ref.dtype)


def _in_proj(x, mod, g, w, wdt, mod_row, tm, tn=1024):
    t, d = x.shape
    n = w.shape[1]
    return pl.pallas_call(
        _in_proj_kernel,
        grid=(t // tm, n // tn),
        in_specs=[
            pl.BlockSpec((tm, d), lambda i, j: (i, 0)),
            pl.BlockSpec((1, 6, d), lambda i, j: (mod_row(i), 0, 0)),
            pl.BlockSpec((1, d), lambda i, j: (0, 0)),
            pl.BlockSpec((d, tn), lambda i, j: (0, j)),
            pl.BlockSpec((d, DT_COLS), lambda i, j: (0, 0)),
        ],
        out_specs=[
            pl.BlockSpec((tm, tn), lambda i, j: (i, j)),
            pl.BlockSpec((tm, DT_COLS), lambda i, j: (i, 0)),
        ],
        out_shape=[
            jax.ShapeDtypeStruct((t, n), BF16),
            jax.ShapeDtypeStruct((t, DT_COLS), F32),
        ],
        scratch_shapes=[pltpu.VMEM((tm, d), BF16)],
        compiler_params=_params(("parallel", "arbitrary")),
        name="in_proj",
    )(x, mod, g, w, wdt)


def _split3(x):
    hi = x.astype(BF16)
    r = x - hi.astype(F32)
    mid = r.astype(BF16)
    lo = (r - mid.astype(F32)).astype(BF16)
    return hi, mid, lo


def _block_diag_pair(x):
    lane = lax.broadcasted_iota(jnp.int32, x.shape, 1)
    return jnp.concatenate([jnp.where(lane < SSD_HEAD_DIM, x, 0.0).astype(BF16),
                            jnp.where(lane >= SSD_HEAD_DIM, x, 0.0).astype(BF16)], axis=0)


def _ssd_kernel(xbc_ref, prev_ref, next_ref, z_ref, dt_ref, s0f_ref, s0b_ref,
                cw_ref, cb_ref, dtb_ref, alog_ref, dskip_ref, gn_ref,
                y_ref, sf_ref, sb_ref,
                ext_ref, xs_ref, ytmp_ref, yb_ref, st_ref, *, n_chunks, zero_init):
    ph = pl.program_id(1)
    c = pl.program_id(2)
    fwd = ph == 1
    cidx = jnp.where(fwd, c, n_chunks - 1 - c)
    sgn = jnp.where(fwd, 1, -1)

    @pl.when(c == 0)
    def _():
        if zero_init:
            st_ref[...] = jnp.zeros_like(st_ref)
        else:
            @pl.when(fwd)
            def _():
                st_ref[...] = s0f_ref[0].T

            @pl.when(jnp.logical_not(fwd))
            def _():
                st_ref[...] = s0b_ref[0].T

    has_prev = (cidx > 0).astype(F32)
    has_next = (cidx < n_chunks - 1).astype(F32)
    ext_ref[0:HALO, :] = prev_ref[...].astype(F32) * has_prev
    ext_ref[HALO:HALO + CHUNK, :] = xbc_ref[...].astype(F32)
    ext_ref[HALO + CHUNK:, :] = next_ref[...].astype(F32) * has_next

    def conv_cols(lo, width):
        acc = jnp.broadcast_to(cb_ref[:, lo:lo + width], (CHUNK, width))
        for k in range(SSD_CONV):
            r0 = HALO + k - SSD_CONV // 2
            acc = acc + ext_ref[r0:r0 + CHUNK, lo:lo + width] * cw_ref[k:k + 1, lo:lo + width]
        return _silu(acc)

    for blk in range(SSD_INNER // 512):
        xs_ref[:, blk * 512:(blk + 1) * 512] = conv_cols(blk * 512, 512)
    bm = conv_cols(SSD_INNER, SSD_BC)
    cm = conv_cols(SSD_INNER + SSD_BC, SSD_BC)

    ii = lax.broadcasted_iota(jnp.int32, (CHUNK, CHUNK), 0)
    jj = lax.broadcasted_iota(jnp.int32, (CHUNK, CHUNK), 1)
    mask = (ii - jj) * sgn >= 0
    tri = jnp.where(mask, 1.0, 0.0).astype(BF16)
    dt = _softplus(dt_ref[...] + dtb_ref[0])
    la = dt * (-jnp.exp(alog_ref[0]))
    la_hi, la_mid, la_lo = _split3(la)
    acum = _dot(tri, la_hi) + _dot(tri, la_mid) + _dot(tri, la_lo)
    acum_t = acum.T
    dt_t = dt.T
    tot_t = jnp.sum(la.T, axis=1, keepdims=True)
    w_end_t = dt_t * jnp.exp(tot_t - acum_t)
    cdecay = jnp.exp(jnp.sum(la, axis=0, keepdims=True))

    lane128 = lax.broadcasted_iota(jnp.int32, (1, 2 * SSD_HEAD_DIM), 1)
    neg_inf = jnp.float32(-jnp.inf)

    for g in range(SSD_GROUPS):
        b_g = bm[:, g * SSD_STATE:(g + 1) * SSD_STATE]
        c_g = cm[:, g * SSD_STATE:(g + 1) * SSD_STATE]
        b_gt = b_g.T
        cb = _dot(c_g.astype(BF16), b_gt.astype(BF16))
        for pair in range(HEADS_PER_GROUP // 2):
            h0 = g * HEADS_PER_GROUP + 2 * pair
            m_parts, b_parts, c_parts = [], [], []
            for h in (h0, h0 + 1):
                colb = jnp.broadcast_to(acum[:, h:h + 1], (CHUNK, CHUNK))
                row = acum_t[h:h + 1, :]
                decay = jnp.exp(jnp.where(mask, colb - row, neg_inf))
                m_parts.append((cb * decay * dt_t[h:h + 1, :]).astype(BF16))
                b_parts.append((b_gt * w_end_t[h:h + 1, :]).astype(BF16))
                c_parts.append((c_g * jnp.exp(colb)).astype(BF16))
            cols = slice(h0 * SSD_HEAD_DIM, (h0 + 2) * SSD_HEAD_DIM)
            bd_x = _block_diag_pair(xs_ref[:, cols])
            s_old = st_ref[:, cols]
            bd_s = _block_diag_pair(s_old)
            lhs_x = jnp.concatenate([jnp.concatenate(m_parts, axis=1),
                                     jnp.concatenate(b_parts, axis=1)], axis=0)
            res = _dot(lhs_x, bd_x)
            y_pair = res[:CHUNK] + _dot(jnp.concatenate(c_parts, axis=1), bd_s)
            cd = jnp.where(lane128 < SSD_HEAD_DIM,
                           jnp.broadcast_to(cdecay[:, h0:h0 + 1], lane128.shape),
                           jnp.broadcast_to(cdecay[:, h0 + 1:h0 + 2], lane128.shape))
            st_ref[:, cols] = s_old * cd + res[CHUNK:]
            ytmp_ref[:, cols] = y_pair

    row0 = pl.multiple_of(cidx * CHUNK, CHUNK)

    @pl.when(jnp.logical_not(fwd))
    def _():
        yb_ref[pl.ds(row0, CHUNK), :] = ytmp_ref[...]

        @pl.when(c == n_chunks - 1)
        def _():
            sb_ref[0] = st_ref[...].T

    @pl.when(fwd)
    def _():
        @pl.when(c == n_chunks - 1)
        def _():
            sf_ref[0] = st_ref[...].T

        gw = SSD_INNER // SSD_GROUPS
        for g in range(SSD_GROUPS):
            cols = slice(g * gw, (g + 1) * gw)
            y = ytmp_ref[:, cols] + yb_ref[pl.ds(row0, CHUNK), cols] + xs_ref[:, cols] * dskip_ref[:, cols]
            y = y * _silu(z_ref[:, cols].astype(F32))
            y_ref[:, cols] = (_rms(y) * gn_ref[:, cols]).astype(y_ref.dtype)


def _ssd(proj, dt, s0f, s0b, cw, cb, dtb, alog, dskip, gn, *, row0, n_seq, seq_len, zero_init):
    nc = seq_len // CHUNK
    blk0 = row0 // CHUNK
    hp = SSD_INNER
    last16 = proj.shape[0] // HALO - 1

    def chunk_of(b, ph, c):
        return blk0 + b * nc + jnp.where(ph == 1, c, nc - 1 - c)

    def out_chunk(b, ph, c):
        return b * nc + jnp.where(ph == 1, c, 0)

    if zero_init:
        s_map = lambda b, ph, c: (0, 0, 0)
    else:
        s_map = lambda b, ph, c: (b, 0, 0)
    const2 = lambda b, ph, c: (0, 0)
    dir3 = lambda b, ph, c: (1 - ph, 0, 0)

    kern = functools.partial(_ssd_kernel, n_chunks=nc, zero_init=zero_init)
    return pl.pallas_call(
        kern,
        grid=(n_seq, 2, nc),
        in_specs=[
            pl.BlockSpec((CHUNK, SSD_CONV_CH), lambda b, ph, c: (chunk_of(b, ph, c), 0)),
            pl.BlockSpec((HALO, SSD_CONV_CH),
                         lambda b, ph, c: (jnp.maximum(chunk_of(b, ph, c) * (CHUNK // HALO) - 1, 0), 0)),
            pl.BlockSpec((HALO, SSD_CONV_CH),
                         lambda b, ph, c: (jnp.minimum((chunk_of(b, ph, c) + 1) * (CHUNK // HALO), last16), 0)),
            pl.BlockSpec((CHUNK, hp), lambda b, ph, c: (blk0 + out_chunk(b, ph, c), COL_Z)),
            pl.BlockSpec((CHUNK, DT_COLS // 2), lambda b, ph, c: (chunk_of(b, ph, c), 1 - ph)),
            pl.BlockSpec((1, hp, SSD_STATE), s_map),
            pl.BlockSpec((1, hp, SSD_STATE), s_map),
            pl.BlockSpec((8, SSD_CONV_CH), const2),
            pl.BlockSpec((1, SSD_CONV_CH), const2),
            pl.BlockSpec((1, 1, DT_COLS // 2), dir3),
            pl.BlockSpec((1, 1, DT_COLS // 2), dir3),
            pl.BlockSpec((1, hp), const2),
            pl.BlockSpec((1, hp), const2),
        ],
        out_specs=[
            pl.BlockSpec((CHUNK, hp), lambda b, ph, c: (out_chunk(b, ph, c), 0)),
            pl.BlockSpec((1, hp, SSD_STATE), lambda b, ph, c: (b, 0, 0)),
            pl.BlockSpec((1, hp, SSD_STATE), lambda b, ph, c: (b, 0, 0)),
        ],
        out_shape=[
            jax.ShapeDtypeStruct((n_seq * seq_len, hp), BF16),
            jax.ShapeDtypeStruct((n_seq, hp, SSD_STATE), F32),
            jax.ShapeDtypeStruct((n_seq, hp, SSD_STATE), F32),
        ],
        scratch_shapes=[
            pltpu.VMEM((CHUNK + 2 * HALO, SSD_CONV_CH), F32),
            pltpu.VMEM((CHUNK, hp), F32),
            pltpu.VMEM((CHUNK, hp), F32),
            pltpu.VMEM((seq_len, hp), F32),
            pltpu.VMEM((SSD_STATE, hp), F32),
        ],
        compiler_params=_params(("parallel", "arbitrary", "arbitrary")),
        name="ssd_zero" if zero_init else "ssd_cached",
    )(proj, proj, proj, proj, dt, s0f, s0b, cw, cb, dtb, alog, dskip, gn)


def _merge_kernel(p_ref, pprev_ref, pnext_ref, u_ref, v_ref, ga_ref, gb_ref, gc_ref,
                  yp_ref, ys_ref, x_ref, mod_ref,
                  wssd_ref, wpoolbr_ref, wgmlp_ref, wout_ref, wpool_ref, pscale_ref,
                  gsgu_ref, wsp_ref, bsp_ref,
                  o_ref, pext_ref, ypool_ref, ygmlp_ref, *, tm, n_prompt_tiles, len_prompt, len_sample):
    i = pl.program_id(0)
    is_prompt = i < n_prompt_tiles
    seq_len = jnp.where(is_prompt, len_prompt, len_sample)

    pext_ref[0:HALO, :] = pprev_ref[...]
    pext_ref[HALO:HALO + tm, :] = p_ref[...]
    pext_ref[HALO + tm:, :] = pnext_ref[...]

    tt = lax.broadcasted_iota(jnp.int32, (CHUNK, CHUNK + 2 * HALO), 0)
    ee = lax.broadcasted_iota(jnp.int32, (CHUNK, CHUNK + 2 * HALO), 1)
    off = ee - HALO - tt
    tcol = lax.broadcasted_iota(jnp.int32, (CHUNK, 1), 0)

    for r in range(tm // CHUNK):
        rows = slice(r * CHUNK, (r + 1) * CHUNK)
        pos0 = (i * tm + r * CHUNK) & (seq_len - 1)
        ext = pext_ref[r * CHUNK:r * CHUNK + CHUNK + 2 * HALO, :]
        src = pos0 + tt + off
        valid = (src >= 0) & (src < seq_len)
        pos = pos0 + tcol
        for gi, w in enumerate(POOL_WINDOWS):
            cols = slice(gi * POOL_GROUP, (gi + 1) * POOL_GROUP)
            band = jnp.where(valid & (off >= -(w // 2)) & (off < w // 2), 1.0, 0.0).astype(BF16)
            cnt = (jnp.minimum(pos + w // 2, seq_len) - jnp.maximum(pos - w // 2, 0)).astype(F32)
            pooled = _dot(band, ext[:, cols]) / cnt - p_ref[rows, cols].astype(F32)
            mixed = _dot(pooled.astype(BF16), wpool_ref[gi])
            ypool_ref[rows, cols] = (mixed * pscale_ref[:, cols]).astype(BF16)
        vn = (_rms(v_ref[rows, :].astype(F32)) * gsgu_ref[...]).astype(BF16)
        for g in range(GMLP_GROUPS):
            cols = slice(g * GMLP_GROUP, (g + 1) * GMLP_GROUP)
            sv = _dot(wsp_ref[g], vn[:, cols]) + bsp_ref[:, cols]
            ygmlp_ref[rows, cols] = (u_ref[rows, cols].astype(F32) * sv).astype(BF16)

    y_ssd = jnp.where(is_prompt, yp_ref[...], ys_ref[...])
    merged = (_sigmoid(ga_ref[...].astype(F32)) * _dot(y_ssd, wssd_ref[...])
              + _sigmoid(gb_ref[...].astype(F32)) * _dot(ypool_ref[...], wpoolbr_ref[...])
              + _sigmoid(gc_ref[...].astype(F32)) * _dot(ygmlp_ref[...], wgmlp_ref[...]))
    o_ref[...] = x_ref[...] + mod_ref[0, 2:3, :] * _dot(merged.astype(BF16), wout_ref[...])


def _merge(proj, y_p, y_s, x, mod, wssd, wpoolbr, wgmlp, wout, wpool, pscale, gsgu, wsp, bsp,
           mod_row, *, tm, n_prompt_tiles, len_prompt, len_sample):
    t, d = x.shape
    last16 = t // HALO - 1
    r16 = tm // HALO
    npt = n_prompt_tiles
    nst = t // tm - npt
    row = lambda col: (lambda i: (i, col))
    const2 = lambda i: (0, 0)
    const3 = lambda i: (0, 0, 0)
    kern = functools.partial(_merge_kernel, tm=tm, n_prompt_tiles=npt,
                             len_prompt=len_prompt, len_sample=len_sample)
    return pl.pallas_call(
        kern,
        grid=(t // tm,),
        in_specs=[
            pl.BlockSpec((tm, d), row(COL_P)),
            pl.BlockSpec((HALO, d), lambda i: (jnp.maximum(i * r16 - 1, 0), COL_P)),
            pl.BlockSpec((HALO, d), lambda i: (jnp.minimum((i + 1) * r16, last16), COL_P)),
            pl.BlockSpec((tm, d), row(COL_U)),
            pl.BlockSpec((tm, d), row(COL_V)),
            pl.BlockSpec((tm, d), row(COL_GATE)),
            pl.BlockSpec((tm, d), row(COL_GATE + 1)),
            pl.BlockSpec((tm, d), row(COL_GATE + 2)),
            pl.BlockSpec((tm, SSD_INNER), lambda i: (jnp.minimum(i, npt - 1), 0)),
            pl.BlockSpec((tm, SSD_INNER), lambda i: (jnp.clip(i - npt, 0, nst - 1), 0)),
            pl.BlockSpec((tm, d), row(0)),
            pl.BlockSpec((1, 6, d), lambda i: (mod_row(i), 0, 0)),
            pl.BlockSpec((SSD_INNER, d), const2),
            pl.BlockSpec((d, d), const2),
            pl.BlockSpec((d, d), const2),
            pl.BlockSpec((d, d), const2),
            pl.BlockSpec((len(POOL_WINDOWS), POOL_GROUP, POOL_GROUP), const3),
            pl.BlockSpec((1, d), const2),
            pl.BlockSpec((1, d), const2),
            pl.BlockSpec((GMLP_GROUPS, CHUNK, CHUNK), const3),
            pl.BlockSpec((CHUNK, d), const2),
        ],
        out_specs=pl.BlockSpec((tm, d), row(0)),
        out_shape=jax.ShapeDtypeStruct((t, d), F32),
        scratch_shapes=[
            pltpu.VMEM((tm + 2 * HALO, d), BF16),
            pltpu.VMEM((tm, d), BF16),
            pltpu.VMEM((tm, d), BF16),
        ],
        compiler_params=_params(("parallel",)),
        name="merge",
    )(proj, proj, proj, proj, proj, proj, proj, proj, y_p, y_s, x, mod,
      wssd, wpoolbr, wgmlp, wout, wpool, pscale, gsgu, wsp, bsp)


def _ffn_kernel(x_ref, mod_ref, g_ref, wg_ref, wu_ref, wo_ref, gf_ref, o_ref, h_ref, acc_ref, *, final_norm):
    k = pl.program_id(1)

    @pl.when(k == 0)
    def _():
        y = _rms(x_ref[...]) * g_ref[...]
        h_ref[...] = (y * (1.0 + mod_ref[0, 4:5, :]) + mod_ref[0, 3:4, :]).astype(BF16)
        acc_ref[...] = jnp.zeros_like(acc_ref)

    h = h_ref[...]
    act = (_silu(_dot(h, wg_ref[...])) * _dot(h, wu_ref[...])).astype(BF16)
    acc_ref[...] += _dot(act, wo_ref[...])

    @pl.when(k == pl.num_programs(1) - 1)
    def _():
        out = x_ref[...] + mod_ref[0, 5:6, :] * acc_ref[...]
        if final_norm:
            out = _rms(out) * gf_ref[...]
        o_ref[...] = out


def _ffn(x, mod, g, w_in, w_out, g_final, mod_row, *, tm, tf, final_norm):
    t, d = x.shape
    nff = D_FF // tf
    return pl.pallas_call(
        functools.partial(_ffn_kernel, final_norm=final_norm),
        grid=(t // tm, nff),
        in_specs=[
            pl.BlockSpec((tm, d), lambda i, k: (i, 0)),
            pl.BlockSpec((1, 6, d), lambda i, k: (mod_row(i), 0, 0)),
            pl.BlockSpec((1, d), lambda i, k: (0, 0)),
            pl.BlockSpec((d, tf), lambda i, k: (0, k)),
            pl.BlockSpec((d, tf), lambda i, k: (0, nff + k)),
            pl.BlockSpec((tf, d), lambda i, k: (k, 0)),
            pl.BlockSpec((1, d), lambda i, k: (0, 0)),
        ],
        out_specs=pl.BlockSpec((tm, d), lambda i, k: (i, 0)),
        out_shape=jax.ShapeDtypeStruct((t, d), F32),
        scratch_shapes=[pltpu.VMEM((tm, d), BF16), pltpu.VMEM((tm, d), F32)],
        compiler_params=_params(("parallel", "arbitrary")),
        name="ffn",
    )(x, mod, g, w_in, w_in, w_out, g_final)


def _grid_pos_embed(n_tokens, dim):
    rows = n_tokens // GRID_W
    quarter = dim // 4
    omega = jnp.exp(-math.log(POS_BASE) * jnp.arange(quarter, dtype=F32) / quarter)
    row = jnp.repeat(jnp.arange(rows, dtype=F32), GRID_W)
    col = jnp.tile(jnp.arange(GRID_W, dtype=F32), rows)

    def axis_emb(p):
        ang = p[:, None] * omega[None, :]
        return jnp.concatenate([jnp.sin(ang), jnp.cos(ang)], axis=-1)

    return jnp.concatenate([axis_emb(row), axis_emb(col)], axis=-1)


def _tile_rows(tp, ts_seq, cap):
    return math.gcd(math.gcd(tp, ts_seq), cap)


def kernel(x_prompt, x_sample, state_ssd_fwd, state_ssd_bwd, c, c_ctx, w_ada, b_ada, g_norm1, w_in, conv_w, conv_b, dt_bias, a_log, d_skip, g_ssd, w_br_ssd, w_pool, pool_scale, w_br_pool, g_sgu, w_spatial, b_spatial, w_br_gmlp, w_out, g_norm2, w_ffn_in, w_ffn_out, g_final):
    batch, seq, d = x_prompt.shape
    dec_batch, dec_seq, _ = x_sample.shape
    depth = w_in.shape[0]
    tp, ts = batch * seq, dec_batch * dec_seq
    tm = _tile_rows(tp, dec_seq, 1024)
    tm_merge = _tile_rows(tp, dec_seq, 256)

    def mod_row_fn(tile):
        npt = tp // tile
        return lambda i: jnp.where(i < npt, 0, 1 + jnp.maximum(i - npt, 0) // (dec_seq // tile))

    o_xbc, o_dt, o_p = SSD_INNER, SSD_INNER + SSD_CONV_CH, SSD_INNER + SSD_CONV_CH + 2 * SSD_HEADS
    o_gate = o_p + 3 * D_MODEL
    w_main = jnp.concatenate([w_in[:, :, o_xbc:o_dt], w_in[:, :, o_p:o_gate], w_in[:, :, :o_xbc],
                              w_in[:, :, o_gate:]], axis=-1).astype(BF16)
    w_dt_raw = w_in[:, :, o_dt:o_p]
    pad = jnp.zeros((depth, d, DT_COLS // 2 - SSD_HEADS), F32)
    w_dt = jnp.concatenate([w_dt_raw[:, :, :SSD_HEADS], pad, w_dt_raw[:, :, SSD_HEADS:], pad], axis=-1).astype(BF16)

    def per_dir(v):
        return jnp.pad(v, ((0, 0), (0, 0), (0, DT_COLS // 2 - SSD_HEADS)))[:, :, None, :]

    dtb, alog = per_dir(dt_bias), per_dir(a_log)
    conv_w8 = jnp.pad(conv_w, ((0, 0), (0, 8 - SSD_CONV), (0, 0)))
    dskip = jnp.repeat(d_skip, SSD_HEAD_DIM, axis=-1)
    bsp = jnp.repeat(jnp.swapaxes(b_spatial, 1, 2), GMLP_GROUP, axis=-1)
    w_br_ssd_b, w_br_pool_b, w_br_gmlp_b = w_br_ssd.astype(BF16), w_br_pool.astype(BF16), w_br_gmlp.astype(BF16)
    w_out_b, w_pool_b, w_sp_b = w_out.astype(BF16), w_pool.astype(BF16), w_spatial.astype(BF16)
    w_ffn_in_b, w_ffn_out_b = w_ffn_in.astype(BF16), w_ffn_out.astype(BF16)

    mod_rows = -(-(1 + dec_batch) // 16) * 16
    cvec = jnp.concatenate([c_ctx[None, :], c, jnp.zeros((mod_rows - 1 - dec_batch, d), F32)], axis=0)
    mods = _modulation(cvec, w_ada, b_ada).reshape(depth, mod_rows, 6, d)

    x = _embed(x_prompt.reshape(tp, d), x_sample.reshape(ts, d), _grid_pos_embed(dec_seq, d), tm)

    hp = SSD_INNER
    s0f = state_ssd_fwd.reshape(dec_batch, depth, hp, SSD_STATE)
    s0b = state_ssd_bwd.reshape(dec_batch, depth, hp, SSD_STATE)
    zero_state = jnp.zeros((1, hp, SSD_STATE), F32)
    fwd_states, bwd_states = [], []
    for l in range(depth):
        proj, dt = _in_proj(x, mods[l], g_norm1[l][None], w_main[l], w_dt[l], mod_row_fn(tm), tm)
        ssd_args = (conv_w8[l], conv_b[l][None], dtb[l], alog[l], dskip[l][None], g_ssd[l][None])
        y_p, s_f, s_b = _ssd(proj, dt, zero_state, zero_state, *ssd_args,
                             row0=0, n_seq=batch, seq_len=seq, zero_init=True)
        y_s, _, _ = _ssd(proj, dt, s0f[:, l], s0b[:, l], *ssd_args,
                         row0=tp, n_seq=dec_batch, seq_len=dec_seq, zero_init=False)
        fwd_states.append(s_f)
        bwd_states.append(s_b)
        x = _merge(proj, y_p, y_s, x, mods[l], w_br_ssd_b[l], w_br_pool_b[l], w_br_gmlp_b[l], w_out_b[l],
                   w_pool_b[l], pool_scale[l][None], g_sgu[l][None], w_sp_b[l], bsp[l], mod_row_fn(tm_merge),
                   tm=tm_merge, n_prompt_tiles=tp // tm_merge, len_prompt=seq, len_sample=dec_seq)
        x = _ffn(x, mods[l], g_norm2[l][None], w_ffn_in_b[l], w_ffn_out_b[l], g_final[None], mod_row_fn(tm),
                 tm=tm, tf=256, final_norm=(l == depth - 1))

    y_prompt = x[:tp].reshape(batch, seq, d)
    y_sample = x[tp:].reshape(dec_batch, dec_seq, d)
    shape5 = (batch, depth, SSD_HEADS, SSD_HEAD_DIM, SSD_STATE)
    new_f = jnp.stack(fwd_states, axis=1).reshape(shape5)
    new_b = jnp.stack(bwd_states, axis=1).reshape(shape5)
    return (y_prompt, y_sample, new_f, new_b)
```

```python
import functools
import math

import jax
import jax.numpy as jnp
import numpy as np
from jax import lax
from jax.experimental import pallas as pl
from jax.experimental.pallas import tpu as pltpu

F32 = jnp.float32
BF16 = jnp.bfloat16

D_MODEL = 1024
GRID_W = 64
POS_BASE = 10000.0
RMS_EPS = 1e-6
SSD_INNER = 2 * D_MODEL
SSD_HEAD_DIM = 64
SSD_HEADS = SSD_INNER // SSD_HEAD_DIM
SSD_GROUPS = 4
SSD_STATE = 128
SSD_CONV = 5
SSD_BC = SSD_GROUPS * SSD_STATE
SSD_CONV_CH = SSD_INNER + 2 * SSD_BC
HEADS_PER_GROUP = SSD_HEADS // SSD_GROUPS
POOL_WINDOWS = (2, 4, 8, 16)
POOL_GROUP = D_MODEL // len(POOL_WINDOWS)
GMLP_GROUPS = 4
GMLP_GROUP = D_MODEL // GMLP_GROUPS
CHUNK = 128
HALO = 16
D_FF = -(-8 * D_MODEL // (3 * 256)) * 256
N_BRANCH = 3

COL_P, COL_U, COL_V = 3, 4, 5
COL_Z = 3
COL_GATE = 8
PROJ_COLS = 11 * D_MODEL
DT_COLS = 256

CONV_HALO_BASE = SSD_CONV * CHUNK
CONV_K = CONV_HALO_BASE + 128

V7X_VMEM_LIMIT = 56 * 1024 * 1024


NEG_LOG2E = -1.4426950408889634


def _silu(x):
    return x / (1.0 + jnp.exp2(x * NEG_LOG2E))


def _sigmoid(x):
    return 1.0 / (1.0 + jnp.exp2(x * NEG_LOG2E))


def _softplus(x):
    return jnp.maximum(x, 0.0) + jnp.log1p(jnp.exp(-jnp.abs(x)))


def _dot(a, b):
    return jnp.dot(a, b, preferred_element_type=F32)


def _rms(x, eps=RMS_EPS):
    return x * lax.rsqrt(jnp.mean(x * x, axis=-1, keepdims=True) + eps)


def _params(sem, vmem=V7X_VMEM_LIMIT):
    return pltpu.CompilerParams(dimension_semantics=sem, vmem_limit_bytes=vmem)


def _embed_kernel(xp_ref, xs_ref, pos_ref, o_ref, *, n_prompt_tiles):
    i = pl.program_id(0)

    @pl.when(i < n_prompt_tiles)
    def _():
        o_ref[...] = xp_ref[...]

    @pl.when(i >= n_prompt_tiles)
    def _():
        o_ref[...] = xs_ref[...] + pos_ref[...]


def _embed(xp, xs, pos, tm):
    tp, d = xp.shape
    ts = xs.shape[0]
    npt, nst, npos = tp // tm, ts // tm, pos.shape[0] // tm
    return pl.pallas_call(
        functools.partial(_embed_kernel, n_prompt_tiles=npt),
        grid=(npt + nst,),
        in_specs=[
            pl.BlockSpec((tm, d), lambda i: (jnp.minimum(i, npt - 1), 0)),
            pl.BlockSpec((tm, d), lambda i: (jnp.maximum(i - npt, 0), 0)),
            pl.BlockSpec((tm, d), lambda i: (jnp.maximum(i - npt, 0) % npos, 0)),
        ],
        out_specs=pl.BlockSpec((tm, d), lambda i: (i, 0)),
        out_shape=jax.ShapeDtypeStruct((tp + ts, d), F32),
        compiler_params=_params(("parallel",)),
        name="embed",
    )(xp, xs, pos)


def _mod_kernel(c_ref, w_ref, b_ref, o_ref):
    a = _silu(c_ref[...]).astype(BF16)
    o_ref[...] = _dot(a, w_ref[...].astype(BF16)) + b_ref[...]


def _modulation(cvec, w_ada, b_ada, tn=1024):
    depth, d, n = w_ada.shape
    rows = cvec.shape[0]
    return pl.pallas_call(
        _mod_kernel,
        grid=(depth, n // tn),
        in_specs=[
            pl.BlockSpec((rows, d), lambda l, j: (0, 0)),
            pl.BlockSpec((None, d, tn), lambda l, j: (l, 0, j)),
            pl.BlockSpec((None, 1, tn), lambda l, j: (l, 0, j)),
        ],
        out_specs=pl.BlockSpec((None, rows, tn), lambda l, j: (l, 0, j)),
        out_shape=jax.ShapeDtypeStruct((depth, rows, n), F32),
        compiler_params=_params(("parallel", "parallel")),
        name="modulation",
    )(cvec, w_ada, b_ada.reshape(depth, 1, n))


def _in_proj_kernel(x_ref, mod_ref, g_ref, w_ref, wdt_ref, o_ref, dt_ref, h_ref):
    @pl.when(pl.program_id(1) == 0)
    def _():
        y = _rms(x_ref[...]) * g_ref[...]
        h = (y * (1.0 + mod_ref[0, 1:2, :]) + mod_ref[0, 0:1, :]).astype(BF16)
        h_ref[...] = h
        dt_ref[...] = _dot(h, wdt_ref[...])

    o_ref[...] = _dot(h_ref[...], w_ref[...]).astype(o_ref.dtype)


def _in_proj(x, mods, g, w, wdt, mod_row, layer, tm, tn=1024):
    t, d = x.shape
    n = w.shape[-1]
    return pl.pallas_call(
        _in_proj_kernel,
        grid=(t // tm, n // tn),
        in_specs=[
            pl.BlockSpec((tm, d), lambda i, j: (i, 0)),
            pl.BlockSpec((None, 1, 6, d), lambda i, j: (layer, mod_row(i), 0, 0)),
            pl.BlockSpec((None, 1, d), lambda i, j: (layer, 0, 0)),
            pl.BlockSpec((None, d, tn), lambda i, j: (layer, 0, j)),
            pl.BlockSpec((None, d, DT_COLS), lambda i, j: (layer, 0, 0)),
        ],
        out_specs=[
            pl.BlockSpec((tm, tn), lambda i, j: (i, j)),
            pl.BlockSpec((tm, DT_COLS), lambda i, j: (i, 0)),
        ],
        out_shape=[
            jax.ShapeDtypeStruct((t, n), BF16),
            jax.ShapeDtypeStruct((t, DT_COLS), F32),
        ],
        scratch_shapes=[pltpu.VMEM((tm, d), BF16)],
        compiler_params=_params(("parallel", "arbitrary")),
        name="in_proj",
    )(x, mods, g, w, wdt)


def _split3(x):
    hi = x.astype(BF16)
    r = x - hi.astype(F32)
    mid = r.astype(BF16)
    lo = (r - mid.astype(F32)).astype(BF16)
    return hi, mid, lo


def _conv_shift_matrix():
    s = np.zeros((CHUNK, CONV_K), np.float32)
    half = SSD_CONV // 2
    for t in range(CHUNK):
        for k in range(SSD_CONV):
            j = t + k - half
            if 0 <= j < CHUNK:
                s[t, k * CHUNK + j] = 1.0
            elif j < 0:
                s[t, CONV_HALO_BASE + k * HALO + (HALO + j)] = 1.0
            else:
                s[t, CONV_HALO_BASE + (k - 1) * HALO + (j - CHUNK)] = 1.0
    return s


def _ssd_kernel(*refs, n_chunks, zero_init, emit_state, alias_state):
    (xbc_ref, prev_ref, next_ref, z_ref, dt_ref, s0f_ref, s0b_ref, shift_ref,
     cw_ref, cbias_ref, dtb_ref, alog_ref, dskip_ref, gn_ref) = refs[:14]
    rest = refs[14 + (2 if alias_state else 0):]
    y_ref = rest[0]
    if emit_state:
        sf_ref, sb_ref = rest[1:3]
    (rhs_ref, xs_c, c_c, bt_c, acum_s, rowp_s, wend_s, cdec_s,
     ytmp_ref, yb_ref, st_ref) = rest[(3 if emit_state else 1):]

    ph = pl.program_id(1)
    c = pl.program_id(2)
    fwd = ph == 1
    cidx = jnp.where(fwd, c, n_chunks - 1 - c)
    sgn = jnp.where(fwd, 1, -1)
    row0 = pl.multiple_of(cidx * CHUNK, CHUNK)
    btrow0 = pl.multiple_of(cidx * SSD_BC, SSD_BC)

    ii = lax.broadcasted_iota(jnp.int32, (CHUNK, CHUNK), 0)
    jj = lax.broadcasted_iota(jnp.int32, (CHUNK, CHUNK), 1)
    mask = (ii - jj) * sgn >= 0

    @pl.when(c == 0)
    def _():
        if zero_init:
            st_ref[...] = jnp.zeros_like(st_ref)
        else:
            @pl.when(fwd)
            def _():
                st_ref[...] = s0f_ref[...].T

            @pl.when(jnp.logical_not(fwd))
            def _():
                st_ref[...] = s0b_ref[...].T

        tri = jnp.where(mask, 1.0, 0.0).astype(BF16)
        neg_a = -jnp.exp(alog_ref[...])
        for k in range(n_chunks):
            rows = slice(k * CHUNK, (k + 1) * CHUNK)
            dt = _softplus(dt_ref[rows, :] + dtb_ref[...])
            la = dt * neg_a
            la_hi, la_mid, la_lo = _split3(la)
            acum = _dot(tri, la_hi) + _dot(tri, la_mid) + _dot(tri, la_lo)
            tot = jnp.sum(la, axis=0, keepdims=True)
            acum_s[rows, :] = acum
            rowp_s[rows, :] = (acum - jnp.log(dt)).T
            wend_s[rows, :] = dt * jnp.exp(tot - acum)
            cdec_s[k * 8:(k + 1) * 8, :] = jnp.broadcast_to(jnp.exp(tot), (8, DT_COLS // 2))

    @pl.when(jnp.logical_not(fwd))
    def _():
        has_prev = (cidx > 0).astype(F32)
        has_next = (cidx < n_chunks - 1).astype(F32)
        xc = xbc_ref[...]
        for k in range(SSD_CONV):
            rhs_ref[k * CHUNK:(k + 1) * CHUNK, :] = xc * cw_ref[k:k + 1, :].astype(BF16)
        pv, nx = prev_ref[...], next_ref[...]
        half = SSD_CONV // 2
        for k in range(half):
            r = CONV_HALO_BASE + k * HALO
            rhs_ref[r:r + HALO, :] = pv * (cw_ref[k:k + 1, :] * has_prev).astype(BF16)
        for k in range(half + 1, SSD_CONV):
            r = CONV_HALO_BASE + (k - 1) * HALO
            rhs_ref[r:r + HALO, :] = nx * (cw_ref[k:k + 1, :] * has_next).astype(BF16)
        fill = CONV_HALO_BASE + (SSD_CONV - 1) * HALO
        rhs_ref[fill:, :] = jnp.zeros((CONV_K - fill, SSD_CONV_CH), BF16)

        shift = shift_ref[...]
        width = SSD_BC
        for blk in range(SSD_CONV_CH // width):
            cols = slice(blk * width, (blk + 1) * width)
            act = _silu(_dot(shift, rhs_ref[:, cols]) + cbias_ref[:, cols])
            if blk < SSD_INNER // width:
                xs_c[pl.ds(row0, CHUNK), cols] = act.astype(BF16)
            elif blk == SSD_INNER // width:
                for g in range(SSD_GROUPS):
                    b_gt = act[:, g * SSD_STATE:(g + 1) * SSD_STATE].T
                    bt_c[pl.ds(btrow0 + g * SSD_STATE, SSD_STATE), :] = b_gt.astype(BF16)
            else:
                c_c[pl.ds(row0, CHUNK), :] = act.astype(BF16)

    acum = acum_s[pl.ds(row0, CHUNK), :]
    rowp = rowp_s[pl.ds(row0, CHUNK), :]
    w_end = wend_s[pl.ds(row0, CHUNK), :]
    cdecay = cdec_s[pl.ds(pl.multiple_of(cidx * 8, 8), 8), :][0:1, :]

    lane_row = lax.broadcasted_iota(jnp.int32, (1, 2 * SSD_HEAD_DIM), 1)
    left = jj < SSD_HEAD_DIM
    keep_l = jnp.where(left, 1.0, 0.0).astype(BF16)
    keep_r = jnp.where(left, 0.0, 1.0).astype(BF16)
    neg_inf = jnp.float32(-jnp.inf)
    gw = HEADS_PER_GROUP * SSD_HEAD_DIM
    pw = 2 * SSD_HEAD_DIM

    def col_of(v, h):
        return jnp.broadcast_to(v[:, h:h + 1], (CHUNK, CHUNK))

    for g in range(SSD_GROUPS):
        gcols = slice(g * gw, (g + 1) * gw)
        c_g = c_c[pl.ds(row0, CHUNK), g * SSD_STATE:(g + 1) * SSD_STATE]
        b_gt = bt_c[pl.ds(btrow0 + g * SSD_STATE, SSD_STATE), :]
        cb = _dot(c_g, b_gt)
        y_off = _dot(c_g, st_ref[:, gcols].astype(BF16))
        xw_parts, cd_parts = [], []
        for pair in range(HEADS_PER_GROUP // 2):
            h0 = g * HEADS_PER_GROUP + 2 * pair
            cols = slice(h0 * SSD_HEAD_DIM, (h0 + 2) * SSD_HEAD_DIM)
            colb0, colb1 = col_of(acum, h0), col_of(acum, h0 + 1)
            m0 = (cb * jnp.exp(jnp.where(mask, colb0 - rowp[h0:h0 + 1, :], neg_inf))).astype(BF16)
            m1 = (cb * jnp.exp(jnp.where(mask, colb1 - rowp[h0 + 1:h0 + 2, :], neg_inf))).astype(BF16)
            x_pair = xs_c[pl.ds(row0, CHUNK), cols]
            x_diag = jnp.concatenate([x_pair * keep_l, x_pair * keep_r], axis=0)
            y_diag = _dot(jnp.concatenate([m0, m1], axis=1), x_diag)
            e_pair = jnp.where(left, jnp.exp(colb0), jnp.exp(colb1))
            ytmp_ref[:, cols] = y_diag + y_off[:, pair * pw:(pair + 1) * pw] * e_pair
            w_pair = jnp.where(left, col_of(w_end, h0), col_of(w_end, h0 + 1))
            xw_parts.append(x_pair * w_pair.astype(BF16))
            cd_parts.append(jnp.where(lane_row < SSD_HEAD_DIM,
                                      jnp.broadcast_to(cdecay[:, h0:h0 + 1], lane_row.shape),
                                      jnp.broadcast_to(cdecay[:, h0 + 1:h0 + 2], lane_row.shape)))
        s_new = _dot(b_gt, jnp.concatenate(xw_parts, axis=1))
        st_ref[:, gcols] = st_ref[:, gcols] * jnp.concatenate(cd_parts, axis=1) + s_new

    @pl.when(jnp.logical_not(fwd))
    def _():
        yb_ref[pl.ds(row0, CHUNK), :] = ytmp_ref[...]
        if emit_state:
            @pl.when(c == n_chunks - 1)
            def _():
                sb_ref[...] = st_ref[...].T

    @pl.when(fwd)
    def _():
        if emit_state:
            @pl.when(c == n_chunks - 1)
            def _():
                sf_ref[...] = st_ref[...].T

        gw = SSD_INNER // SSD_GROUPS
        for g in range(SSD_GROUPS):
            cols = slice(g * gw, (g + 1) * gw)
            xs = xs_c[pl.ds(row0, CHUNK), cols].astype(F32)
            y = ytmp_ref[:, cols] + yb_ref[pl.ds(row0, CHUNK), cols] + xs * dskip_ref[:, cols]
            y = y * _silu(z_ref[:, cols].astype(F32))
            y_ref[:, cols] = (_rms(y) * gn_ref[:, cols]).astype(y_ref.dtype)


def _ssd(proj, dt, s0f, s0b, shift, cw, cbias, dtb, alog, dskip, gn, prev_states, *, layer, s0_layer,
         row0, n_seq, seq_len, zero_init, emit_state, depth):
    nc = seq_len // CHUNK
    blk0 = row0 // CHUNK
    hp = SSD_INNER
    last16 = proj.shape[0] // HALO - 1
    r16 = CHUNK // HALO

    assert row0 % seq_len == 0
    seq0 = row0 // seq_len

    def conv_chunk(b, ph, c):
        return blk0 + b * nc + jnp.where(ph == 1, 0, nc - 1 - c)

    def out_chunk(b, ph, c):
        return b * nc + jnp.where(ph == 1, c, 0)

    s_map = (lambda b, ph, c: (0, 0, 0, 0)) if zero_init else (lambda b, ph, c: (b, s0_layer, 0, 0))
    lay3 = lambda b, ph, c: (layer, 0, 0)
    dir4 = lambda b, ph, c: (layer, 1 - ph, 0, 0)
    alias_state = emit_state and prev_states is not None

    in_specs = [
        pl.BlockSpec((CHUNK, SSD_CONV_CH), lambda b, ph, c: (conv_chunk(b, ph, c), 0)),
        pl.BlockSpec((HALO, SSD_CONV_CH),
                     lambda b, ph, c: (jnp.maximum(conv_chunk(b, ph, c) * r16 - 1, 0), 0)),
        pl.BlockSpec((HALO, SSD_CONV_CH),
                     lambda b, ph, c: (jnp.minimum((conv_chunk(b, ph, c) + 1) * r16, last16), 0)),
        pl.BlockSpec((CHUNK, hp), lambda b, ph, c: (blk0 + out_chunk(b, ph, c), COL_Z)),
        pl.BlockSpec((seq_len, DT_COLS // 2), lambda b, ph, c: (seq0 + b, 1 - ph)),
        pl.BlockSpec((None, None, hp, SSD_STATE), s_map),
        pl.BlockSpec((None, None, hp, SSD_STATE), s_map),
        pl.BlockSpec((CHUNK, CONV_K), lambda b, ph, c: (0, 0)),
        pl.BlockSpec((None, 8, SSD_CONV_CH), lay3),
        pl.BlockSpec((None, 1, SSD_CONV_CH), lay3),
        pl.BlockSpec((None, None, 1, DT_COLS // 2), dir4),
        pl.BlockSpec((None, None, 1, DT_COLS // 2), dir4),
        pl.BlockSpec((None, 1, hp), lay3),
        pl.BlockSpec((None, 1, hp), lay3),
    ]
    args = [proj, proj, proj, proj, dt, s0f, s0b, shift, cw, cbias, dtb, alog, dskip, gn]
    out_specs = [pl.BlockSpec((CHUNK, hp), lambda b, ph, c: (out_chunk(b, ph, c), 0))]
    out_shape = [jax.ShapeDtypeStruct((n_seq * seq_len, hp), BF16)]
    aliases = {}
    if emit_state:
        state_spec = pl.BlockSpec((None, None, hp, SSD_STATE), lambda b, ph, c: (b, layer, 0, 0))
        out_specs += [state_spec, state_spec]
        out_shape += [jax.ShapeDtypeStruct((n_seq, depth, hp, SSD_STATE), F32)] * 2
        if alias_state:
            in_specs += [pl.BlockSpec(memory_space=pl.ANY)] * 2
            aliases = {len(args): 1, len(args) + 1: 2}
            args += list(prev_states)

    kern = functools.partial(_ssd_kernel, n_chunks=nc, zero_init=zero_init, emit_state=emit_state,
                             alias_state=alias_state)
    return pl.pallas_call(
        kern,
        grid=(n_seq, 2, nc),
        in_specs=in_specs,
        out_specs=out_specs,
        out_shape=out_shape,
        input_output_aliases=aliases,
        scratch_shapes=[
            pltpu.VMEM((CONV_K, SSD_CONV_CH), BF16),
            pltpu.VMEM((seq_len, hp), BF16),
            pltpu.VMEM((seq_len, SSD_BC), BF16),
            pltpu.VMEM((nc * SSD_BC, CHUNK), BF16),
            pltpu.VMEM((seq_len, DT_COLS // 2), F32),
            pltpu.VMEM((seq_len, DT_COLS // 2), F32),
            pltpu.VMEM((seq_len, DT_COLS // 2), F32),
            pltpu.VMEM((nc * 8, DT_COLS // 2), F32),
            pltpu.VMEM((CHUNK, hp), F32),
            pltpu.VMEM((seq_len, hp), F32),
            pltpu.VMEM((SSD_STATE, hp), F32),
        ],
        compiler_params=_params(("parallel", "arbitrary", "arbitrary")),
        name="ssd_prompt" if zero_init else "ssd_sample",
    )(*args)


def _merge_kernel(p_ref, pprev_ref, pnext_ref, u_ref, v_ref, ga_ref, gb_ref, gc_ref,
                  yp_ref, ys_ref, x_ref, mod_ref,
                  wssd_ref, wpoolbr_ref, wgmlp_ref, wout_ref, wpool_ref, pscale_ref,
                  gsgu_ref, wsp_ref, bsp_ref,
                  o_ref, pext_ref, ypool_ref, ygmlp_ref, *, tm, n_prompt_tiles, len_prompt, len_sample):
    i = pl.program_id(0)
    is_prompt = i < n_prompt_tiles
    seq_len = jnp.where(is_prompt, len_prompt, len_sample)

    pext_ref[0:HALO, :] = pprev_ref[...]
    pext_ref[HALO:HALO + tm, :] = p_ref[...]
    pext_ref[HALO + tm:, :] = pnext_ref[...]

    tt = lax.broadcasted_iota(jnp.int32, (CHUNK, CHUNK + 2 * HALO), 0)
    ee = lax.broadcasted_iota(jnp.int32, (CHUNK, CHUNK + 2 * HALO), 1)
    off = ee - HALO - tt
    tcol = lax.broadcasted_iota(jnp.int32, (CHUNK, 1), 0)

    for r in range(tm // CHUNK):
        rows = slice(r * CHUNK, (r + 1) * CHUNK)
        pos0 = (i * tm + r * CHUNK) & (seq_len - 1)
        ext = pext_ref[r * CHUNK:r * CHUNK + CHUNK + 2 * HALO, :]
        src = pos0 + tt + off
        valid = (src >= 0) & (src < seq_len)
        pos = pos0 + tcol
        for gi, w in enumerate(POOL_WINDOWS):
            cols = slice(gi * POOL_GROUP, (gi + 1) * POOL_GROUP)
            band = jnp.where(valid & (off >= -(w // 2)) & (off < w // 2), 1.0, 0.0).astype(BF16)
            cnt = (jnp.minimum(pos + w // 2, seq_len) - jnp.maximum(pos - w // 2, 0)).astype(F32)
            pooled = _dot(band, ext[:, cols]) / cnt - p_ref[rows, cols].astype(F32)
            mixed = _dot(pooled.astype(BF16), wpool_ref[gi])
            ypool_ref[rows, cols] = (mixed * pscale_ref[:, cols]).astype(BF16)
        vn = (_rms(v_ref[rows, :].astype(F32)) * gsgu_ref[...]).astype(BF16)
        for g in range(GMLP_GROUPS):
            cols = slice(g * GMLP_GROUP, (g + 1) * GMLP_GROUP)
            sv = _dot(wsp_ref[g], vn[:, cols]) + bsp_ref[:, cols]
            ygmlp_ref[rows, cols] = (u_ref[rows, cols].astype(F32) * sv).astype(BF16)

    y_ssd = jnp.where(is_prompt, yp_ref[...], ys_ref[...])
    merged = (_sigmoid(ga_ref[...].astype(F32)) * _dot(y_ssd, wssd_ref[...])
              + _sigmoid(gb_ref[...].astype(F32)) * _dot(ypool_ref[...], wpoolbr_ref[...])
              + _sigmoid(gc_ref[...].astype(F32)) * _dot(ygmlp_ref[...], wgmlp_ref[...]))
    o_ref[...] = x_ref[...] + mod_ref[0, 2:3, :] * _dot(merged.astype(BF16), wout_ref[...])


def _merge(proj, y_p, y_s, x, mods, wssd, wpoolbr, wgmlp, wout, wpool, pscale, gsgu, wsp, bsp,
           mod_row, *, layer, tm, n_prompt_tiles, len_prompt, len_sample):
    t, d = x.shape
    last16 = t // HALO - 1
    r16 = tm // HALO
    npt = n_prompt_tiles
    nst = t // tm - npt
    row = lambda col: (lambda i: (i, col))
    lay3 = lambda i: (layer, 0, 0)
    lay4 = lambda i: (layer, 0, 0, 0)
    kern = functools.partial(_merge_kernel, tm=tm, n_prompt_tiles=npt,
                             len_prompt=len_prompt, len_sample=len_sample)
    return pl.pallas_call(
        kern,
        grid=(t // tm,),
        in_specs=[
            pl.BlockSpec((tm, d), row(COL_P)),
            pl.BlockSpec((HALO, d), lambda i: (jnp.maximum(i * r16 - 1, 0), COL_P)),
            pl.BlockSpec((HALO, d), lambda i: (jnp.minimum((i + 1) * r16, last16), COL_P)),
            pl.BlockSpec((tm, d), row(COL_U)),
            pl.BlockSpec((tm, d), row(COL_V)),
            pl.BlockSpec((tm, d), row(COL_GATE)),
            pl.BlockSpec((tm, d), row(COL_GATE + 1)),
            pl.BlockSpec((tm, d), row(COL_GATE + 2)),
            pl.BlockSpec((tm, SSD_INNER), lambda i: (jnp.minimum(i, npt - 1), 0)),
            pl.BlockSpec((tm, SSD_INNER), lambda i: (jnp.clip(i - npt, 0, nst - 1), 0)),
            pl.BlockSpec((tm, d), row(0)),
            pl.BlockSpec((None, 1, 6, d), lambda i: (layer, mod_row(i), 0, 0)),
            pl.BlockSpec((None, SSD_INNER, d), lay3),
            pl.BlockSpec((None, d, d), lay3),
            pl.BlockSpec((None, d, d), lay3),
            pl.BlockSpec((None, d, d), lay3),
            pl.BlockSpec((None, len(POOL_WINDOWS), POOL_GROUP, POOL_GROUP), lay4),
            pl.BlockSpec((None, 1, d), lay3),
            pl.BlockSpec((None, 1, d), lay3),
            pl.BlockSpec((None, GMLP_GROUPS, CHUNK, CHUNK), lay4),
            pl.BlockSpec((None, CHUNK, d), lay3),
        ],
        out_specs=pl.BlockSpec((tm, d), row(0)),
        out_shape=jax.ShapeDtypeStruct((t, d), F32),
        scratch_shapes=[
            pltpu.VMEM((tm + 2 * HALO, d), BF16),
            pltpu.VMEM((tm, d), BF16),
            pltpu.VMEM((tm, d), BF16),
        ],
        compiler_params=_params(("parallel",)),
        name="merge",
    )(proj, proj, proj, proj, proj, proj, proj, proj, y_p, y_s, x, mods,
      wssd, wpoolbr, wgmlp, wout, wpool, pscale, gsgu, wsp, bsp)


def _ffn_kernel(x_ref, mod_ref, g_ref, wg_ref, wu_ref, wo_ref, gf_ref, o_ref, h_ref, acc_ref, *, final_norm):
    k = pl.program_id(1)

    @pl.when(k == 0)
    def _():
        y = _rms(x_ref[...]) * g_ref[...]
        h_ref[...] = (y * (1.0 + mod_ref[0, 4:5, :]) + mod_ref[0, 3:4, :]).astype(BF16)
        acc_ref[...] = jnp.zeros_like(acc_ref)

    h = h_ref[...]
    act = (_silu(_dot(h, wg_ref[...])) * _dot(h, wu_ref[...])).astype(BF16)
    acc_ref[...] += _dot(act, wo_ref[...])

    @pl.when(k == pl.num_programs(1) - 1)
    def _():
        out = x_ref[...] + mod_ref[0, 5:6, :] * acc_ref[...]
        if final_norm:
            out = _rms(out) * gf_ref[...]
        o_ref[...] = out


def _ffn(x, mods, g, w_in, w_out, g_final, mod_row, *, layer, tm, tf, final_norm):
    t, d = x.shape
    nff = D_FF // tf
    return pl.pallas_call(
        functools.partial(_ffn_kernel, final_norm=final_norm),
        grid=(t // tm, nff),
        in_specs=[
            pl.BlockSpec((tm, d), lambda i, k: (i, 0)),
            pl.BlockSpec((None, 1, 6, d), lambda i, k: (layer, mod_row(i), 0, 0)),
            pl.BlockSpec((None, 1, d), lambda i, k: (layer, 0, 0)),
            pl.BlockSpec((None, d, tf), lambda i, k: (layer, 0, k)),
            pl.BlockSpec((None, d, tf), lambda i, k: (layer, 0, nff + k)),
            pl.BlockSpec((None, tf, d), lambda i, k: (layer, k, 0)),
            pl.BlockSpec((1, d), lambda i, k: (0, 0)),
        ],
        out_specs=pl.BlockSpec((tm, d), lambda i, k: (i, 0)),
        out_shape=jax.ShapeDtypeStruct((t, d), F32),
        scratch_shapes=[pltpu.VMEM((tm, d), BF16), pltpu.VMEM((tm, d), F32)],
        compiler_params=_params(("parallel", "arbitrary")),
        name="ffn",
    )(x, mods, g, w_in, w_in, w_out, g_final)


def _grid_pos_embed(n_tokens, dim):
    rows = n_tokens // GRID_W
    quarter = dim // 4
    omega = jnp.exp(-math.log(POS_BASE) * jnp.arange(quarter, dtype=F32) / quarter)
    row = jnp.repeat(jnp.arange(rows, dtype=F32), GRID_W)
    col = jnp.tile(jnp.arange(GRID_W, dtype=F32), rows)

    def axis_emb(p):
        ang = p[:, None] * omega[None, :]
        return jnp.concatenate([jnp.sin(ang), jnp.cos(ang)], axis=-1)

    return jnp.concatenate([axis_emb(row), axis_emb(col)], axis=-1)


def _tile_rows(tp, ts_seq, cap):
    return math.gcd(math.gcd(tp, ts_seq), cap)


def kernel(x_prompt, x_sample, state_ssd_fwd, state_ssd_bwd, c, c_ctx, w_ada, b_ada, g_norm1, w_in, conv_w, conv_b, dt_bias, a_log, d_skip, g_ssd, w_br_ssd, w_pool, pool_scale, w_br_pool, g_sgu, w_spatial, b_spatial, w_br_gmlp, w_out, g_norm2, w_ffn_in, w_ffn_out, g_final):
    batch, seq, d = x_prompt.shape
    dec_batch, dec_seq, _ = x_sample.shape
    depth = w_in.shape[0]
    tp, ts = batch * seq, dec_batch * dec_seq
    tm = _tile_rows(tp, dec_seq, 1024)
    tm_merge = _tile_rows(tp, dec_seq, 256)

    def mod_row_fn(tile):
        npt = tp // tile
        return lambda i: jnp.where(i < npt, 0, 1 + jnp.maximum(i - npt, 0) // (dec_seq // tile))

    o_xbc, o_dt, o_p = SSD_INNER, SSD_INNER + SSD_CONV_CH, SSD_INNER + SSD_CONV_CH + 2 * SSD_HEADS
    o_gate = o_p + 3 * D_MODEL
    w_main = jnp.concatenate([w_in[:, :, o_xbc:o_dt], w_in[:, :, o_p:o_gate], w_in[:, :, :o_xbc],
                              w_in[:, :, o_gate:]], axis=-1).astype(BF16)
    w_dt_raw = w_in[:, :, o_dt:o_p]
    pad = jnp.zeros((depth, d, DT_COLS // 2 - SSD_HEADS), F32)
    w_dt = jnp.concatenate([w_dt_raw[:, :, :SSD_HEADS], pad, w_dt_raw[:, :, SSD_HEADS:], pad], axis=-1).astype(BF16)

    def per_dir(v):
        return jnp.pad(v, ((0, 0), (0, 0), (0, DT_COLS // 2 - SSD_HEADS)))[:, :, None, :]

    row3 = lambda v: v[:, None, :]
    dtb, alog = per_dir(dt_bias), per_dir(a_log)
    conv_w8 = jnp.pad(conv_w, ((0, 0), (0, 8 - SSD_CONV), (0, 0)))
    dskip = row3(jnp.repeat(d_skip, SSD_HEAD_DIM, axis=-1))
    bsp = jnp.repeat(jnp.swapaxes(b_spatial, 1, 2), GMLP_GROUP, axis=-1)
    w_br_ssd_b, w_br_pool_b, w_br_gmlp_b = w_br_ssd.astype(BF16), w_br_pool.astype(BF16), w_br_gmlp.astype(BF16)
    w_out_b, w_pool_b, w_sp_b = w_out.astype(BF16), w_pool.astype(BF16), w_spatial.astype(BF16)
    w_ffn_in_b, w_ffn_out_b = w_ffn_in.astype(BF16), w_ffn_out.astype(BF16)
    shift = jnp.asarray(_conv_shift_matrix(), BF16)

    mod_rows = -(-(1 + dec_batch) // 16) * 16
    cvec = jnp.concatenate([c_ctx[None, :], c, jnp.zeros((mod_rows - 1 - dec_batch, d), F32)], axis=0)
    mods = _modulation(cvec, w_ada, b_ada).reshape(depth, mod_rows, 6, d)

    x = _embed(x_prompt.reshape(tp, d), x_sample.reshape(ts, d), _grid_pos_embed(dec_seq, d), tm)

    hp = SSD_INNER
    s0f = state_ssd_fwd.reshape(dec_batch, depth, hp, SSD_STATE)
    s0b = state_ssd_bwd.reshape(dec_batch, depth, hp, SSD_STATE)
    zero_state = jnp.zeros((1, 1, hp, SSD_STATE), F32)
    states = None
    for l in range(depth):
        proj, dt = _in_proj(x, mods, row3(g_norm1), w_main, w_dt, mod_row_fn(tm), l, tm)
        ssd_params = (shift, conv_w8, row3(conv_b), dtb, alog, dskip, row3(g_ssd))
        y_p, *states = _ssd(proj, dt, zero_state, zero_state, *ssd_params, states, layer=l, s0_layer=0,
                            row0=0, n_seq=batch, seq_len=seq, zero_init=True, emit_state=True, depth=depth)
        (y_s,) = _ssd(proj, dt, s0f, s0b, *ssd_params, None, layer=l, s0_layer=l,
                      row0=tp, n_seq=dec_batch, seq_len=dec_seq, zero_init=False, emit_state=False, depth=depth)
        x = _merge(proj, y_p, y_s, x, mods, w_br_ssd_b, w_br_pool_b, w_br_gmlp_b, w_out_b,
                   w_pool_b, row3(pool_scale), row3(g_sgu), w_sp_b, bsp, mod_row_fn(tm_merge),
                   layer=l, tm=tm_merge, n_prompt_tiles=tp // tm_merge, len_prompt=seq, len_sample=dec_seq)
        x = _ffn(x, mods, row3(g_norm2), w_ffn_in_b, w_ffn_out_b, g_final[None], mod_row_fn(tm),
                 layer=l, tm=tm, tf=256, final_norm=(l == depth - 1))

    y_prompt = x[:tp].reshape(batch, seq, d)
    y_sample = x[tp:].reshape(dec_batch, dec_seq, d)
    shape5 = (batch, depth, SSD_HEADS, SSD_HEAD_DIM, SSD_STATE)
    return (y_prompt, y_sample, states[0].reshape(shape5), states[1].reshape(shape5))
```
